```python
import jax, jax.numpy as jnp
from jax import lax
import numpy as np

D_MODEL = 2048
BATCH = 2
SEQ = 4096
DEPTH = 4
DEC_BATCH = 8
DEC_SEQ = 8
PAST_LEN = 16384
PAGE_SIZE = 128

N_MIXERS = 2
N_POOL_LAYERS = (DEPTH + 1) // 2
N_ATTN_LAYERS = DEPTH // 2
RMS_EPS = 1e-6
POOL_EXPAND = 2
POOL_WIDTH = POOL_EXPAND * D_MODEL
POOL_WINDOWS = (2, 4, 8, 16)
POOL_N_GROUPS = len(POOL_WINDOWS)
POOL_GROUP = POOL_WIDTH // POOL_N_GROUPS
POOL_PREFIX = max(POOL_WINDOWS) - 1
HEAD_DIM = 128
N_HEADS = D_MODEL // HEAD_DIM
ATTN_WIDTH = N_HEADS * HEAD_DIM
DILATED_GROUPS = ((128, 1), (512, 4), (2048, 16))
N_GROUPS = len(DILATED_GROUPS)
ATTN_IN = N_GROUPS * 3 * ATTN_WIDTH + ATTN_WIDTH
ATTN_SCALE = HEAD_DIM ** -0.5

kernel_name = 'hybrid_pool_dilated_attn_decoder_step'


def rmsnorm(x, g):
    xf = x.astype(jnp.float32)
    r = lax.rsqrt(jnp.mean(xf * xf, axis=-1, keepdims=True) + RMS_EPS)
    return (xf * r * g.astype(jnp.float32)).astype(x.dtype)


def pool_branch(h, prefix, pos0, w_in, w_grp, scale, w_out):
    proj = jnp.einsum('nld,de->nle', h, w_in)
    u, z = proj[..., :POOL_WIDTH], proj[..., POOL_WIDTH:]
    N, L, _ = u.shape
    ext = jnp.concatenate([prefix.astype(u.dtype), u], axis=1)
    cs = jnp.concatenate([jnp.zeros((N, 1, POOL_WIDTH), jnp.float32),
                          jnp.cumsum(ext.astype(jnp.float32), axis=1)], axis=1)
    pos = pos0 + jnp.arange(L)
    uf = u.astype(jnp.float32)
    mixed = []
    for gi, w in enumerate(POOL_WINDOWS):
        c = slice(gi * POOL_GROUP, (gi + 1) * POOL_GROUP)
        win_sum = (cs[:, POOL_PREFIX + 1:POOL_PREFIX + 1 + L, c]
                   - cs[:, POOL_PREFIX + 1 - w:POOL_PREFIX + 1 - w + L, c])
        cnt = jnp.minimum(pos + 1, w).astype(jnp.float32)[None, :, None]
        mixed.append(win_sum / cnt - uf[..., c])
    mixed = jnp.stack(mixed, axis=2).astype(u.dtype)
    y = jnp.einsum('nlgc,gcd->nlgd', mixed, w_grp).reshape(N, L, POOL_WIDTH) * scale
    out = jnp.einsum('nle,ed->nld', y * jax.nn.silu(z), w_out)
    return out, ext[:, ext.shape[1] - POOL_PREFIX:]


def dilated_attn_prompt(q, k, v, dil, n_back):
    N, S, H, E = q.shape
    Lc = S // dil
    nb = -(-Lc // n_back)
    Lp = nb * n_back

    def to_classes(t):
        t = t.reshape(N, Lc, dil, H, E).transpose(0, 2, 1, 3, 4)
        t = jnp.pad(t, ((0, 0), (0, 0), (0, Lp - Lc), (0, 0), (0, 0)))
        return t.reshape(N, dil, nb, n_back, H, E).astype(jnp.float32)

    def band(t):
        prev = jnp.pad(t, ((0, 0), (0, 0), (1, 0), (0, 0), (0, 0), (0, 0)))[:, :, :-1]
        return jnp.concatenate([prev, t], axis=3)

    qc = to_classes(q)
    kb, vb = band(to_classes(k)), band(to_classes(v))
    s = jnp.einsum('nrbqhe,nrbkhe->nrbhqk', qc, kb) * ATTN_SCALE
    i = jnp.arange(n_back)[:, None]
    c = jnp.arange(2 * n_back)[None, :]
    blk = jnp.arange(nb)[:, None, None]
    valid = (c >= i) & (c <= i + n_back) & (blk * n_back + c - n_back >= 0)
    s = jnp.where(valid[None, None, :, None], s, -jnp.inf)
    m = jnp.max(s, axis=-1, keepdims=True)
    p = jnp.exp(s - m)
    den = jnp.sum(p, axis=-1, keepdims=True)
    o = jnp.einsum('nrbhqk,nrbkhe->nrbqhe', p / den, vb)
    lse = (m + jnp.log(den))[..., 0]
    o = o.reshape(N, dil, Lp, H, E)[:, :, :Lc].transpose(0, 2, 1, 3, 4).reshape(N, S, H, E)
    lse = lse.transpose(0, 1, 2, 4, 3).reshape(N, dil, Lp, H)[:, :, :Lc]
    lse = lse.transpose(0, 2, 1, 3).reshape(N, S, H)
    return o, lse


def dilated_attn_sample(q, kv_ext, n_buf, dil, n_back):
    N, T, H, E = q.shape
    idx = n_buf + jnp.arange(T)[:, None] - dil * jnp.arange(n_back + 1)[None, :]
    valid = idx >= 0
    g = kv_ext[:, jnp.maximum(idx, 0)].astype(jnp.float32)
    s = jnp.einsum('nthe,ntkhe->nhtk', q.astype(jnp.float32), g[:, :, :, 0]) * ATTN_SCALE
    s = jnp.where(valid[None, None], s, -jnp.inf)
    m = jnp.max(s, axis=-1, keepdims=True)
    p = jnp.exp(s - m)
    den = jnp.sum(p, axis=-1, keepdims=True)
    o = jnp.einsum('nhtk,ntkhe->nthe', p / den, g[:, :, :, 1])
    lse = (m + jnp.log(den))[..., 0].transpose(0, 2, 1)
    return o, lse


def attn_branch(h, kv_bufs, w_in, w_out):
    N, L, _ = h.shape
    proj = jnp.einsum('nld,de->nle', h, w_in)
    outs, lses, new_bufs = [], [], []
    for gi, (win, dil) in enumerate(DILATED_GROUPS):
        n_back = win // dil
        base = gi * 3 * ATTN_WIDTH
        q, k, v = (proj[..., base + j * ATTN_WIDTH:base + (j + 1) * ATTN_WIDTH].reshape(N, L, N_HEADS, HEAD_DIM)
                   for j in range(3))
        kv = jnp.stack([k, v], axis=2)
        if kv_bufs is None:
            o, lse = dilated_attn_prompt(q, k, v, dil, n_back)
            new = kv[:, L - min(win, L):]
        else:
            buf = kv_bufs[gi]
            kv_ext = jnp.concatenate([buf.astype(kv.dtype), kv], axis=1)
            o, lse = dilated_attn_sample(q, kv_ext, buf.shape[1], dil, n_back)
            n_ext = kv_ext.shape[1]
            new = kv_ext[:, n_ext - min(win, n_ext):]
        outs.append(o)
        lses.append(lse)
        new_bufs.append(new)
    wts = jax.nn.softmax(jnp.stack(lses, axis=0), axis=0)[..., None]
    o = jnp.sum(wts * jnp.stack(outs, axis=0), axis=0).astype(h.dtype).reshape(N, L, ATTN_WIDTH)
    gate = proj[..., N_GROUPS * 3 * ATTN_WIDTH:]
    out = jnp.einsum('nlc,cd->nld', o * jax.nn.silu(gate), w_out)
    return out, new_bufs


def setup_inputs(seed: int = 0) -> dict:
    key = jax.random.key(seed)
    ks = iter(jax.random.split(key, 24))
    nrm = lambda shape, scale=1.0: scale * jax.random.normal(next(ks), shape, jnp.float32)
    pool_shape = (DEC_BATCH, POOL_PREFIX, POOL_WIDTH)
    kv_buf = lambda win: nrm((DEC_BATCH, min(win, PAST_LEN), 2, N_HEADS, HEAD_DIM))
    w0, w1, w2 = (g[0] for g in DILATED_GROUPS)
    return {
        'x_prompt': nrm((BATCH, SEQ, D_MODEL)),
        'x_sample': nrm((DEC_BATCH, DEC_SEQ, D_MODEL)),
        'state_pool_l0': nrm(pool_shape),
        'cache_kv_l1_w128': kv_buf(w0),
        'cache_kv_l1_w512': kv_buf(w1),
        'cache_kv_l1_w2048': kv_buf(w2),
        'state_pool_l2': nrm(pool_shape),
        'cache_kv_l3_w128': kv_buf(w0),
        'cache_kv_l3_w512': kv_buf(w1),
        'cache_kv_l3_w2048': kv_buf(w2),
        'norm_pre': 1.0 + nrm((DEPTH, D_MODEL), 0.02),
        'norm_post': 1.0 + nrm((DEPTH, D_MODEL), 0.02),
        'w_in_pool': nrm((N_POOL_LAYERS, D_MODEL, 2 * POOL_WIDTH), D_MODEL ** -0.5),
        'w_grp_pool': nrm((N_POOL_LAYERS, POOL_N_GROUPS, POOL_GROUP, POOL_GROUP), POOL_GROUP ** -0.5),
        'scale_pool': 1.0 + nrm((N_POOL_LAYERS, POOL_WIDTH), 0.1),
        'w_out_pool': nrm((N_POOL_LAYERS, POOL_WIDTH, D_MODEL), POOL_WIDTH ** -0.5),
        'w_in_attn': nrm((N_ATTN_LAYERS, D_MODEL, ATTN_IN), D_MODEL ** -0.5),
        'w_out_attn': nrm((N_ATTN_LAYERS, ATTN_WIDTH, D_MODEL), ATTN_WIDTH ** -0.5),
    }


def reference(x_prompt, x_sample, state_pool_l0, cache_kv_l1_w128, cache_kv_l1_w512, cache_kv_l1_w2048,
              state_pool_l2, cache_kv_l3_w128, cache_kv_l3_w512, cache_kv_l3_w2048,
              norm_pre, norm_post, w_in_pool, w_grp_pool, scale_pool, w_out_pool, w_in_attn, w_out_attn):
    pool_states = {0: state_pool_l0, 2: state_pool_l2}
    attn_caches = {1: (cache_kv_l1_w128, cache_kv_l1_w512, cache_kv_l1_w2048),
                   3: (cache_kv_l3_w128, cache_kv_l3_w512, cache_kv_l3_w2048)}
    xp, xs = x_prompt, x_sample
    new = {}
    for i in range(DEPTH):
        j = i // N_MIXERS
        hp = rmsnorm(xp, norm_pre[i])
        hs = rmsnorm(xs, norm_pre[i])
        if i % N_MIXERS == 0:
            zero_prefix = jnp.zeros((hp.shape[0], POOL_PREFIX, POOL_WIDTH), hp.dtype)
            op, st_p = pool_branch(hp, zero_prefix, 0, w_in_pool[j], w_grp_pool[j], scale_pool[j], w_out_pool[j])
            osm, st_s = pool_branch(hs, pool_states[i], PAST_LEN, w_in_pool[j], w_grp_pool[j], scale_pool[j], w_out_pool[j])
            new['pool_l%d_prompt' % i] = st_p
            new['pool_l%d_sample' % i] = st_s
        else:
            op, bufs_p = attn_branch(hp, None, w_in_attn[j], w_out_attn[j])
            osm, bufs_s = attn_branch(hs, attn_caches[i], w_in_attn[j], w_out_attn[j])
            for (win, _), bp, bs in zip(DILATED_GROUPS, bufs_p, bufs_s):
                new['kv_l%d_w%d_prompt' % (i, win)] = bp
                new['kv_l%d_w%d_sample' % (i, win)] = bs
        xp = xp + rmsnorm(op, norm_post[i])
        xs = xs + rmsnorm(osm, norm_post[i])
    y_prompt, y_sample = xp, xs
    return (y_prompt, y_sample,
            new['pool_l0_prompt'], new['pool_l0_sample'],
            new['kv_l1_w128_prompt'], new['kv_l1_w128_sample'],
            new['kv_l1_w512_prompt'], new['kv_l1_w512_sample'],
            new['kv_l1_w2048_prompt'], new['kv_l1_w2048_sample'],
            new['pool_l2_prompt'], new['pool_l2_sample'],
            new['kv_l3_w128_prompt'], new['kv_l3_w128_sample'],
            new['kv_l3_w512_prompt'], new['kv_l3_w512_sample'],
            new['kv_l3_w2048_prompt'], new['kv_l3_w2048_sample'])
```

```python
import functools

import jax
import jax.numpy as jnp
from jax import lax
from jax.experimental import pallas as pl
from jax.experimental.pallas import tpu as pltpu

F32 = jnp.float32
BF16 = jnp.bfloat16

RMS_EPS = 1e-6
HEAD_DIM = 128
POOL_WINDOWS = (2, 4, 8, 16)
POOL_PREFIX = max(POOL_WINDOWS) - 1
DILATED_GROUPS = ((128, 1), (512, 4), (2048, 16))
PAST_LEN = 16384
ATTN_SCALE = HEAD_DIM ** -0.5

LANES = 128
SUBLANES = 8
V7X_VMEM_LIMIT_BYTES = 56 * 1024 * 1024

ROW_TILE = 1024
COL_TILE = 1024
POOL_COL_TILE = 512
OUT_ROW_TILE = 512
ATTN_BLOCK = 128
COMBINE_ROW_TILE = 256
HALO = POOL_PREFIX + 1
PAD = SUBLANES


def _params(n_axes):
    return pltpu.CompilerParams(
        dimension_semantics=("arbitrary",) * n_axes,
        vmem_limit_bytes=V7X_VMEM_LIMIT_BYTES,
    )


def _col_tile(ncols, col0, cap):
    tn = min(ncols, cap) // LANES * LANES
    while ncols % tn or col0 % tn:
        tn -= LANES
    return tn


def _silu(x):
    return x * (1.0 / (1.0 + jnp.exp(-x)))


def _rms_scale(x):
    return lax.rsqrt(jnp.mean(x * x, axis=-1, keepdims=True) + RMS_EPS)


def _rms_kernel(x_ref, g_ref, h_ref):
    x = x_ref[...]
    h_ref[...] = (x * _rms_scale(x) * g_ref[...]).astype(h_ref.dtype)


def _rms_pre(x, g):
    m, d = x.shape
    tm = min(m, OUT_ROW_TILE)
    return pl.pallas_call(
        _rms_kernel,
        out_shape=jax.ShapeDtypeStruct((m, d), BF16),
        grid=(m // tm,),
        in_specs=[pl.BlockSpec((tm, d), lambda i: (i, 0)), pl.BlockSpec((1, d), lambda i: (0, 0))],
        out_specs=pl.BlockSpec((tm, d), lambda i: (i, 0)),
        compiler_params=_params(1),
        name="rms_pre",
    )(x, g.reshape(1, d))


def _proj_kernel(h_ref, w_ref, *refs, act, dil, tail_rows, tail_first_tile, tiles_per_batch):
    refs = list(refs)
    out_ref = refs.pop(0)
    tail_ref = refs.pop(0) if tail_rows else None
    wbf_ref = refs.pop(0)
    acc_ref = refs.pop(0) if dil > 1 else None
    i = pl.program_id(1)

    @pl.when(i == 0)
    def _():
        wbf_ref[...] = w_ref[...].astype(BF16)

    acc = jnp.dot(h_ref[...], wbf_ref[...], preferred_element_type=F32)
    tm, tn = acc.shape
    if act == "silu":
        acc = _silu(acc)

    if dil == 1 and len(out_ref.shape) == 3:
        out_ref[0] = acc.astype(out_ref.dtype)
    elif dil == 1:
        out_ref[...] = acc.astype(out_ref.dtype)
    else:
        for c in range(tn // LANES):
            acc_ref[c] = acc[:, c * LANES:(c + 1) * LANES]
        for r in range(dil):
            for c in range(tn // LANES):
                out_ref[r, :, c * LANES:(c + 1) * LANES] = (
                    acc_ref[c, pl.ds(r, tm // dil, stride=dil), :].astype(out_ref.dtype))

    if tail_rows:
        heads = tn // HEAD_DIM

        @pl.when(i % tiles_per_batch >= tail_first_tile)
        def _():
            for hl in range(heads):
                tail_ref[:, hl, :] = acc[tm - tail_rows:, hl * HEAD_DIM:(hl + 1) * HEAD_DIM]


def _proj(h, w, layer, col0, ncols, *, act=None, out_dtype=BF16, dil=1, rows_per_batch=None,
          tail_window=0, n_heads=None, name="proj"):
    m, k = h.shape
    tm = min(m, ROW_TILE)
    cap = min(COL_TILE, n_heads * HEAD_DIM) if tail_window else COL_TILE
    tn = _col_tile(ncols, col0, cap)
    assert m % tm == 0
    ni, nj = m // tm, ncols // tn
    jb = col0 // tn
    classes = rows_per_batch is not None
    tiles_per_batch = 1
    if classes:
        assert rows_per_batch % tm == 0 and tm % dil == 0
        tiles_per_batch = rows_per_batch // tm
        nb = m // rows_per_batch
        out_shape = [jax.ShapeDtypeStruct((nb, dil, rows_per_batch // dil, ncols), out_dtype)]
        out_specs = [pl.BlockSpec((None, dil, tm // dil, tn),
                                  lambda j, i: (i // tiles_per_batch, 0, i % tiles_per_batch, j))]
    else:
        assert dil == 1
        out_shape = [jax.ShapeDtypeStruct((m, ncols), out_dtype)]
        out_specs = [pl.BlockSpec((tm, tn), lambda j, i: (i, j))]
    scratch = [pltpu.VMEM((k, tn), BF16)]
    if dil > 1:
        scratch.append(pltpu.VMEM((tn // LANES, tm, LANES), F32))
    tail_rows = tail_first_tile = 0
    if tail_window:
        assert classes and tail_window <= rows_per_batch
        heads_per_block = tn // HEAD_DIM
        width = n_heads * HEAD_DIM
        assert width % tn == 0 and (heads_per_block == n_heads or heads_per_block % SUBLANES == 0)
        blocks_per_part = width // tn
        tail_rows = min(tail_window, tm)
        assert tail_window % tail_rows == 0
        tail_first_tile = tiles_per_batch - tail_window // tail_rows
        out_shape.append(jax.ShapeDtypeStruct(
            (nb, tail_window, ncols // width, n_heads, HEAD_DIM), F32))
        out_specs.append(pl.BlockSpec(
            (None, tail_rows, None, heads_per_block, HEAD_DIM),
            lambda j, i: (i // tiles_per_batch,
                          jnp.maximum(i % tiles_per_batch - tail_first_tile, 0),
                          j // blocks_per_part, j % blocks_per_part, 0)))
    kern = functools.partial(_proj_kernel, act=act, dil=dil, tail_rows=tail_rows,
                             tail_first_tile=tail_first_tile, tiles_per_batch=tiles_per_batch)
    outs = pl.pallas_call(
        kern,
        out_shape=out_shape,
        grid=(nj, ni),
        in_specs=[pl.BlockSpec((tm, k), lambda j, i: (i, 0)),
                  pl.BlockSpec((None, k, tn), lambda j, i: (layer, 0, jb + j))],
        out_specs=out_specs,
        scratch_shapes=scratch,
        compiler_params=_params(2),
        name=name,
    )(h, w)
    return outs if tail_window else outs[0]


def _pool_kernel(h_ref, w_ref, prefix_ref, mixed_ref, state_ref, wbf_ref, ext_ref, lvl_a, lvl_b, *,
                 seg, nseg, tiles_per_batch, blocks_per_group, pos0):
    j = pl.program_id(0)
    i = pl.program_id(1)
    tile_in_batch = i % tiles_per_batch

    @pl.when(i == 0)
    def _():
        wbf_ref[...] = w_ref[...].astype(BF16)

    u = jnp.dot(h_ref[...], wbf_ref[...], preferred_element_type=F32)
    tn = u.shape[1]
    end = PAD + HALO + seg
    zeros = jnp.zeros((PAD, tn), F32)
    for s in range(nseg):
        if nseg == 1:
            @pl.when(tile_in_batch == 0)
            def _():
                ext_ref[0, PAD:PAD + HALO, :] = prefix_ref[...]

            @pl.when(tile_in_batch != 0)
            def _():
                ext_ref[0, PAD:PAD + HALO, :] = ext_ref[0, end - HALO:end, :]
        else:
            ext_ref[s, PAD:PAD + HALO, :] = prefix_ref[s]
        ext_ref[s, 0:PAD, :] = zeros
        lvl_a[s, 0:PAD, :] = zeros
        lvl_b[s, 0:PAD, :] = zeros
        ext_ref[s, PAD + HALO:end, :] = u[s * seg:(s + 1) * seg]
        if nseg == 1:
            state_ref[...] = ext_ref[0, end - HALO:end, :]
        else:
            state_ref[s] = ext_ref[s, end - HALO:end, :]

    group = j // blocks_per_group
    first = PAD + HALO
    for gi, w in enumerate(POOL_WINDOWS):
        @pl.when(group == gi)
        def _(w=w):
            for s in range(nseg):
                src, bufs, k = ext_ref, (lvl_a, lvl_b), 1
                while 2 * k < w:
                    dst = bufs[0]
                    dst[s, PAD:end, :] = src[s, PAD:end, :] + src[s, PAD - k:end - k, :]
                    src, bufs, k = dst, (bufs[1], bufs[0]), 2 * k
                win_sum = src[s, first:end, :] + src[s, first - k:end - k, :]
                if pos0 + 1 >= w:
                    inv_cnt = 1.0 / w
                else:
                    row = lax.broadcasted_iota(jnp.int32, (seg, 1), 0)
                    pos = pos0 + tile_in_batch * seg + row
                    inv_cnt = 1.0 / jnp.minimum(pos + 1, w).astype(F32)
                mixed = win_sum * inv_cnt - ext_ref[s, first:end, :]
                mixed_ref[s * seg:(s + 1) * seg, :] = mixed.astype(mixed_ref.dtype)


def _pool_u(h, w_in, layer, prefix16, rows_per_batch, pos0):
    m, k = h.shape
    pw = w_in.shape[2] // 2
    nb = m // rows_per_batch
    group_cols = pw // len(POOL_WINDOWS)
    tm = min(m, ROW_TILE)
    tn = min(group_cols, POOL_COL_TILE)
    assert m % tm == 0 and group_cols % tn == 0
    if rows_per_batch >= tm:
        assert rows_per_batch % tm == 0
        seg, nseg, tiles_per_batch = tm, 1, rows_per_batch // tm
        prefix_spec = pl.BlockSpec((None, HALO, tn), lambda j, i: (i // tiles_per_batch, 0, j))
    else:
        assert tm % rows_per_batch == 0
        seg, nseg, tiles_per_batch = rows_per_batch, tm // rows_per_batch, 1
        prefix_spec = pl.BlockSpec((nseg, HALO, tn), lambda j, i: (i, 0, j))
    kern = functools.partial(_pool_kernel, seg=seg, nseg=nseg, tiles_per_batch=tiles_per_batch,
                             blocks_per_group=group_cols // tn, pos0=pos0)
    rows = PAD + HALO + seg
    return pl.pallas_call(
        kern,
        out_shape=[jax.ShapeDtypeStruct((m, pw), BF16), jax.ShapeDtypeStruct((nb, HALO, pw), F32)],
        grid=(pw // tn, m // tm),
        in_specs=[pl.BlockSpec((tm, k), lambda j, i: (i, 0)),
                  pl.BlockSpec((None, k, tn), lambda j, i: (layer, 0, j)),
                  prefix_spec],
        out_specs=[pl.BlockSpec((tm, tn), lambda j, i: (i, j)), prefix_spec],
        scratch_shapes=[pltpu.VMEM((k, tn), BF16)] + [pltpu.VMEM((nseg, rows, tn), F32)] * 3,
        compiler_params=_params(2),
        name="pool_u",
    )(h, w_in, prefix16)


def _grouped_kernel(x_ref, w_ref, scale_ref, sz_ref, out_ref, wbf_ref):
    @pl.when(pl.program_id(1) == 0)
    def _():
        wbf_ref[...] = w_ref[...].astype(BF16)

    y = jnp.dot(x_ref[...], wbf_ref[...], preferred_element_type=F32)
    out_ref[...] = (y * scale_ref[...] * sz_ref[...]).astype(out_ref.dtype)


def _grouped(mixed, w_grp, layer, scale, sz):
    m, pw = mixed.shape
    gc = w_grp.shape[-1]
    tm = min(m, ROW_TILE)
    tn = min(gc, COL_TILE)
    per_group = gc // tn
    return pl.pallas_call(
        _grouped_kernel,
        out_shape=jax.ShapeDtypeStruct((m, pw), BF16),
        grid=(pw // tn, m // tm),
        in_specs=[pl.BlockSpec((tm, gc), lambda j, i: (i, j // per_group)),
                  pl.BlockSpec((None, None, gc, tn), lambda j, i: (layer, j // per_group, 0, j % per_group)),
                  pl.BlockSpec((1, tn), lambda j, i: (0, j)),
                  pl.BlockSpec((tm, tn), lambda j, i: (i, j))],
        out_specs=pl.BlockSpec((tm, tn), lambda j, i: (i, j)),
        scratch_shapes=[pltpu.VMEM((gc, tn), BF16)],
        compiler_params=_params(2),
        name="pool_grouped",
    )(mixed, w_grp, scale.reshape(1, pw), sz)


def _out_kernel(t_ref, w_ref, x_ref, gpost_ref, *refs, has_next):
    if has_next:
        gnext_ref, xo_ref, ho_ref = refs
    else:
        (xo_ref,) = refs
    o = jnp.dot(t_ref[...], w_ref[...], preferred_element_type=F32)
    xn = x_ref[...] + o * _rms_scale(o) * gpost_ref[...]
    xo_ref[...] = xn
    if has_next:
        ho_ref[...] = (xn * _rms_scale(xn) * gnext_ref[...]).astype(ho_ref.dtype)


def _out_proj(t, w_bf, layer, x, g_post, g_next):
    m, kt = t.shape
    d = x.shape[1]
    tm = min(m, OUT_ROW_TILE)
    has_next = g_next is not None
    row = lambda i: (i, 0)
    vec = pl.BlockSpec((1, d), lambda i: (0, 0))
    in_specs = [pl.BlockSpec((tm, kt), row),
                pl.BlockSpec((None, kt, d), lambda i: (layer, 0, 0), pipeline_mode=pl.Buffered(1)),
                pl.BlockSpec((tm, d), row), vec]
    args = [t, w_bf, x, g_post.reshape(1, d)]
    out_shape = [jax.ShapeDtypeStruct((m, d), F32)]
    out_specs = [pl.BlockSpec((tm, d), row)]
    if has_next:
        in_specs.append(vec)
        args.append(g_next.reshape(1, d))
        out_shape.append(jax.ShapeDtypeStruct((m, d), BF16))
        out_specs.append(pl.BlockSpec((tm, d), row))
    outs = pl.pallas_call(
        functools.partial(_out_kernel, has_next=has_next),
        out_shape=out_shape,
        grid=(m // tm,),
        in_specs=in_specs,
        out_specs=out_specs,
        compiler_params=_params(1),
        name="out_proj",
    )(*args)
    return (outs[0], outs[1]) if has_next else (outs[0], None)


def _softmax_pv(scores, values, masks):
    scores = [jnp.where(mk, s * ATTN_SCALE, -jnp.inf) for s, mk in zip(scores, masks)]
    mx = functools.reduce(jnp.maximum, [jnp.max(s, axis=-1, keepdims=True) for s in scores])
    ps = [jnp.exp(s - mx) for s in scores]
    den = functools.reduce(jnp.add, [jnp.sum(p, axis=-1, keepdims=True) for p in ps])
    o = functools.reduce(jnp.add, [jnp.dot(p.astype(BF16), v, preferred_element_type=F32)
                                   for p, v in zip(ps, values)])
    return o / den, mx + jnp.log(den)


def _qkt(q, k):
    return lax.dot_general(q, k, (((1,), (1,)), ((), ())), preferred_element_type=F32)


def _attn_prompt_kernel(q_ref, kc_ref, vc_ref, kp_ref, vp_ref, o_ref, lse_ref, *, n_heads):
    b = pl.program_id(2)
    shape = (ATTN_BLOCK, ATTN_BLOCK)
    row = lax.broadcasted_iota(jnp.int32, shape, 0)
    col = lax.broadcasted_iota(jnp.int32, shape, 1)
    mask_cur = col <= row
    mask_prev = (col >= row) & (b > 0)
    lane = lax.broadcasted_iota(jnp.int32, (ATTN_BLOCK, LANES), 1)
    lse_tile = jnp.zeros((ATTN_BLOCK, LANES), F32)
    for h in range(n_heads):
        sl = slice(h * HEAD_DIM, (h + 1) * HEAD_DIM)
        q = q_ref[:, sl]
        o, lse = _softmax_pv([_qkt(q, kc_ref[:, sl]), _qkt(q, kp_ref[:, sl])],
                             [vc_ref[:, sl], vp_ref[:, sl]], [mask_cur, mask_prev])
        o_ref[:, sl] = o
        lse_tile = jnp.where(lane == h, lse, lse_tile)
    lse_ref[...] = lse_tile


def _attn_prompt(q, kv, n_heads):
    nb, dil, lc, d = q.shape
    assert lc % ATTN_BLOCK == 0
    blk = (None, None, ATTN_BLOCK, d)
    prev = lambda b: jnp.maximum(b - 1, 0)
    return pl.pallas_call(
        functools.partial(_attn_prompt_kernel, n_heads=n_heads),
        out_shape=[jax.ShapeDtypeStruct((nb, dil, lc, d), F32),
                   jax.ShapeDtypeStruct((nb, dil, lc, LANES), F32)],
        grid=(nb, dil, lc // ATTN_BLOCK),
        in_specs=[pl.BlockSpec(blk, lambda n, r, b: (n, r, b, 0)),
                  pl.BlockSpec(blk, lambda n, r, b: (n, r, b, 0)),
                  pl.BlockSpec(blk, lambda n, r, b: (n, r, b, 1)),
                  pl.BlockSpec(blk, lambda n, r, b: (n, r, prev(b), 0)),
                  pl.BlockSpec(blk, lambda n, r, b: (n, r, prev(b), 1))],
        out_specs=[pl.BlockSpec(blk, lambda n, r, b: (n, r, b, 0)),
                   pl.BlockSpec((None, None, ATTN_BLOCK, LANES), lambda n, r, b: (n, r, b, 0))],
        compiler_params=_params(3),
        name="attn_prompt",
    )(q, kv, kv, kv, kv)


def _attn_sample_kernel(q_ref, kn_ref, vn_ref, kc_ref, vc_ref, o_ref, lse_ref, *, dil, n_back, heads):
    t_len = q_ref.shape[0]
    n_buf = kc_ref.shape[0]
    head0 = pl.program_id(1) * heads
    t_c = lax.broadcasted_iota(jnp.int32, (t_len, n_buf), 0)
    c_c = lax.broadcasted_iota(jnp.int32, (t_len, n_buf), 1)
    back_c = n_buf + t_c - c_c
    mask_c = ((back_c & (dil - 1)) == 0) & (back_c <= dil * n_back)
    t_n = lax.broadcasted_iota(jnp.int32, (t_len, t_len), 0)
    c_n = lax.broadcasted_iota(jnp.int32, (t_len, t_len), 1)
    back_n = t_n - c_n
    mask_n = (back_n >= 0) & ((back_n & (dil - 1)) == 0) & (back_n <= dil * n_back)
    lane = lax.broadcasted_iota(jnp.int32, (t_len, LANES), 1)
    lse_tile = jnp.zeros((t_len, LANES), F32)
    for hl in range(heads):
        q = q_ref[:, hl, :].astype(BF16)
        kc = kc_ref[:, hl, :].astype(BF16)
        kn = kn_ref[:, hl, :].astype(BF16)
        o, lse = _softmax_pv([_qkt(q, kc), _qkt(q, kn)],
                             [vc_ref[:, hl, :].astype(BF16), vn_ref[:, hl, :].astype(BF16)],
                             [mask_c, mask_n])
        o_ref[:, hl, :] = o
        lse_tile = jnp.where(lane == head0 + hl, lse, lse_tile)
    lse_ref[...] = lse_tile


def _attn_sample(q, kv_new, cache, dil, n_back):
    nb, t_len, n_heads, _ = q.shape
    n_buf = cache.shape[1]
    heads = min(n_heads, SUBLANES)
    assert n_heads % heads == 0 and dil & (dil - 1) == 0
    n_hb = n_heads // heads
    new_blk = (None, t_len, None, heads, HEAD_DIM)
    buf_blk = (None, n_buf, None, heads, HEAD_DIM)
    o, lse = pl.pallas_call(
        functools.partial(_attn_sample_kernel, dil=dil, n_back=n_back, heads=heads),
        out_shape=[jax.ShapeDtypeStruct((nb, t_len, n_heads, HEAD_DIM), F32),
                   jax.ShapeDtypeStruct((nb, n_hb, t_len, LANES), F32)],
        grid=(nb, n_hb),
        in_specs=[pl.BlockSpec((None, t_len, heads, HEAD_DIM), lambda n, hb: (n, 0, hb, 0)),
                  pl.BlockSpec(new_blk, lambda n, hb: (n, 0, 0, hb, 0)),
                  pl.BlockSpec(new_blk, lambda n, hb: (n, 0, 1, hb, 0)),
                  pl.BlockSpec(buf_blk, lambda n, hb: (n, 0, 0, hb, 0)),
                  pl.BlockSpec(buf_blk, lambda n, hb: (n, 0, 1, hb, 0))],
        out_specs=[pl.BlockSpec((None, t_len, heads, HEAD_DIM), lambda n, hb: (n, 0, hb, 0)),
                   pl.BlockSpec((None, None, t_len, LANES), lambda n, hb: (n, hb, 0, 0))],
        compiler_params=_params(2),
        name="attn_sample",
    )(q, kv_new, kv_new, cache, cache)
    return o, jnp.sum(lse, axis=1)


def _combine_kernel(*refs, dils, n_heads):
    ng = len(dils)
    o_refs, lse_refs = refs[:ng], refs[ng:2 * ng]
    gate_ref, t_ref, o_pos, lse_pos = refs[2 * ng:]
    tm = t_ref.shape[0]
    for g, dil in enumerate(dils):
        for r in range(dil):
            rows = pl.ds(r, tm // dil, stride=dil) if dil > 1 else slice(None)
            lse_pos[g, rows, :] = lse_refs[g][r]
            for h in range(n_heads):
                o_pos[g, h, rows, :] = o_refs[g][r, :, h * HEAD_DIM:(h + 1) * HEAD_DIM]
    lses = [lse_pos[g] for g in range(ng)]
    mx = functools.reduce(jnp.maximum, lses)
    es = [jnp.exp(l - mx) for l in lses]
    den = functools.reduce(jnp.add, es)
    ws = [e / den for e in es]
    for h in range(n_heads):
        sl = slice(h * HEAD_DIM, (h + 1) * HEAD_DIM)
        mix = functools.reduce(jnp.add, [ws[g][:, h:h + 1] * o_pos[g, h] for g in range(ng)])
        t_ref[:, sl] = (mix * gate_ref[:, sl]).astype(t_ref.dtype)


def _combine(os_, lses, gate, n_heads):
    nb, seq, d = gate.shape
    dils = tuple(o.shape[1] for o in os_)
    tm = min(seq, COMBINE_ROW_TILE)
    assert seq % tm == 0 and all(tm % dl == 0 for dl in dils)
    in_specs = [pl.BlockSpec((None, dl, tm // dl, d), lambda n, i: (n, 0, i, 0)) for dl in dils]
    in_specs += [pl.BlockSpec((None, dl, tm // dl, LANES), lambda n, i: (n, 0, i, 0)) for dl in dils]
    in_specs.append(pl.BlockSpec((None, tm, d), lambda n, i: (n, i, 0)))
    return pl.pallas_call(
        functools.partial(_combine_kernel, dils=dils, n_heads=n_heads),
        out_shape=jax.ShapeDtypeStruct((nb, seq, d), BF16),
        grid=(nb, seq // tm),
        in_specs=in_specs,
        out_specs=pl.BlockSpec((None, tm, d), lambda n, i: (n, i, 0)),
        scratch_shapes=[pltpu.VMEM((len(dils), n_heads, tm, HEAD_DIM), F32),
                        pltpu.VMEM((len(dils), tm, LANES), F32)],
        compiler_params=_params(2),
        name="attn_combine",
    )(*os_, *lses, gate)


def _shift_kernel(*refs, n_arrays, keeps):
    caches, news = refs[:n_arrays], refs[n_arrays:2 * n_arrays]
    outs, sem = refs[2 * n_arrays:3 * n_arrays], refs[3 * n_arrays]
    copies = []
    for a in range(n_arrays):
        n_buf, t_len, keep = caches[a].shape[1], news[a].shape[1], keeps[a]
        copies.append(pltpu.make_async_copy(
            caches[a].at[:, pl.ds(n_buf - keep, keep)], outs[a].at[:, pl.ds(0, keep)], sem.at[2 * a]))
        copies.append(pltpu.make_async_copy(
            news[a], outs[a].at[:, pl.ds(keep, t_len)], sem.at[2 * a + 1]))
    for c in copies:
        c.start()
    for c in copies:
        c.wait()


def _shift_buffers(caches, news, windows):
    keeps, shapes = [], []
    for c, nw, win in zip(caches, news, windows):
        n_buf, t_len = c.shape[1], nw.shape[1]
        out_rows = min(win, n_buf + t_len)
        assert out_rows > t_len
        keeps.append(out_rows - t_len)
        shapes.append(jax.ShapeDtypeStruct((c.shape[0], out_rows) + c.shape[2:], c.dtype))
    n = len(caches)
    any_spec = pl.BlockSpec(memory_space=pl.ANY)
    return pl.pallas_call(
        functools.partial(_shift_kernel, n_arrays=n, keeps=tuple(keeps)),
        out_shape=shapes,
        in_specs=[any_spec] * (2 * n),
        out_specs=[any_spec] * n,
        scratch_shapes=[pltpu.SemaphoreType.DMA((2 * n,))],
        name="shift_buffers",
    )(*caches, *news)


def _pool_layer(h, x, rows_per_batch, prefix16, pos0, layer, w_in, w_grp, scale, w_out_bf, g_post, g_next):
    pw = w_in.shape[2] // 2
    mixed, state16 = _pool_u(h, w_in, layer, prefix16, rows_per_batch, pos0)
    sz = _proj(h, w_in, layer, pw, pw, act="silu", out_dtype=F32, name="pool_z")
    t = _grouped(mixed, w_grp, layer, scale[layer], sz)
    x, h = _out_proj(t, w_out_bf, layer, x, g_post, g_next)
    return x, h, state16[:, 1:]


def _attn_layer_prompt(h, x, nb, seq, n_heads, layer, w_in, w_out_bf, g_post, g_next):
    d = x.shape[1]
    os_, lses, bufs = [], [], []
    for gi, (win, dil) in enumerate(DILATED_GROUPS):
        assert win // dil == ATTN_BLOCK
        base = gi * 3 * d
        q = _proj(h, w_in, layer, base, d, dil=dil, rows_per_batch=seq, name="attn_q")
        kv, tail = _proj(h, w_in, layer, base + d, 2 * d, dil=dil, rows_per_batch=seq,
                         tail_window=min(win, seq), n_heads=n_heads, name="attn_kv")
        o, lse = _attn_prompt(q, kv, n_heads)
        os_.append(o)
        lses.append(lse)
        bufs.append(tail)
    gate = _proj(h, w_in, layer, len(DILATED_GROUPS) * 3 * d, d, act="silu", out_dtype=F32, name="attn_gate")
    t = _combine(os_, lses, gate.reshape(nb, seq, d), n_heads)
    x, h = _out_proj(t.reshape(nb * seq, d), w_out_bf, layer, x, g_post, g_next)
    return x, h, bufs


def _attn_layer_sample(h, x, nb, t_len, n_heads, caches, layer, w_in, w_out_bf, g_post, g_next):
    d = x.shape[1]
    ng = len(DILATED_GROUPS)
    proj = _proj(h, w_in, layer, 0, ng * 3 * d, out_dtype=F32, name="attn_qkv_sample")
    gate = _proj(h, w_in, layer, ng * 3 * d, d, act="silu", out_dtype=F32, name="attn_gate")
    os_, lses, news = [], [], []
    for gi, (win, dil) in enumerate(DILATED_GROUPS):
        base = gi * 3 * d
        q = proj[:, base:base + d].reshape(nb, t_len, n_heads, HEAD_DIM)
        kv_new = proj[:, base + d:base + 3 * d].reshape(nb, t_len, 2, n_heads, HEAD_DIM)
        o, lse = _attn_sample(q, kv_new, caches[gi], dil, win // dil)
        os_.append(o.reshape(nb, 1, t_len, d))
        lses.append(lse.reshape(nb, 1, t_len, LANES))
        news.append(kv_new)
    bufs = _shift_buffers(caches, news, [g[0] for g in DILATED_GROUPS])
    t = _combine(os_, lses, gate.reshape(nb, t_len, d), n_heads)
    x, h = _out_proj(t.reshape(nb * t_len, d), w_out_bf, layer, x, g_post, g_next)
    return x, h, bufs


def kernel(x_prompt, x_sample, state_pool_l0, cache_kv_l1_w128, cache_kv_l1_w512, cache_kv_l1_w2048,
           state_pool_l2, cache_kv_l3_w128, cache_kv_l3_w512, cache_kv_l3_w2048,
           norm_pre, norm_post, w_in_pool, w_grp_pool, scale_pool, w_out_pool, w_in_attn, w_out_attn):
    nbp, seq, d = x_prompt.shape
    nbs, t_len, _ = x_sample.shape
    depth = norm_pre.shape[0]
    n_heads = d // HEAD_DIM
    pw = w_in_pool.shape[2] // 2
    pool_states = {0: state_pool_l0, 2: state_pool_l2}
    attn_caches = {1: (cache_kv_l1_w128, cache_kv_l1_w512, cache_kv_l1_w2048),
                   3: (cache_kv_l3_w128, cache_kv_l3_w512, cache_kv_l3_w2048)}
    w_out_pool_bf = w_out_pool.astype(BF16)
    w_out_attn_bf = w_out_attn.astype(BF16)

    xp = x_prompt.reshape(nbp * seq, d)
    xs = x_sample.reshape(nbs * t_len, d)
    hp = _rms_pre(xp, norm_pre[0])
    hs = _rms_pre(xs, norm_pre[0])
    new = {}
    for i in range(depth):
        j = i // 2
        g_post = norm_post[i]
        g_next = norm_pre[i + 1] if i + 1 < depth else None
        if i % 2 == 0:
            zero_prefix = jnp.zeros((nbp, HALO, pw), F32)
            xp, hp, st_p = _pool_layer(hp, xp, seq, zero_prefix, 0, j, w_in_pool, w_grp_pool,
                                       scale_pool, w_out_pool_bf, g_post, g_next)
            prefix = jnp.pad(pool_states[i], ((0, 0), (HALO - POOL_PREFIX, 0), (0, 0)))
            xs, hs, st_s = _pool_layer(hs, xs, t_len, prefix, PAST_LEN, j, w_in_pool, w_grp_pool,
                                       scale_pool, w_out_pool_bf, g_post, g_next)
            new["pool_l%d_prompt" % i], new["pool_l%d_sample" % i] = st_p, st_s
        else:
            xp, hp, bufs_p = _attn_layer_prompt(hp, xp, nbp, seq, n_heads, j, w_in_attn,
                                                w_out_attn_bf, g_post, g_next)
            xs, hs, bufs_s = _attn_layer_sample(hs, xs, nbs, t_len, n_heads, attn_caches[i], j,
                                                w_in_attn, w_out_attn_bf, g_post, g_next)
            for (win, _), bp, bs in zip(DILATED_GROUPS, bufs_p, bufs_s):
                new["kv_l%d_w%d_prompt" % (i, win)] = bp
                new["kv_l%d_w%d_sample" % (i, win)] = bs
    outs = [xp.reshape(nbp, seq, d), xs.reshape(nbs, t_len, d)]
    for i in range(depth):
        if i % 2 == 0:
            outs += [new["pool_l%d_prompt" % i], new["pool_l%d_sample" % i]]
        else:
            for win, _ in DILATED_GROUPS:
                outs += [new["kv_l%d_w%d_prompt" % (i, win)], new["kv_l%d_w%d_sample" % (i, win)]]
    return tuple(outs)
```

```python
import functools

import jax
import jax.numpy as jnp
from jax import lax
from jax.experimental import pallas as pl
from jax.experimental.pallas import tpu as pltpu

F32 = jnp.float32
BF16 = jnp.bfloat16

RMS_EPS = 1e-6
HEAD_DIM = 128
POOL_WINDOWS = (2, 4, 8, 16)
POOL_PREFIX = max(POOL_WINDOWS) - 1
DILATED_GROUPS = ((128, 1), (512, 4), (2048, 16))
PAST_LEN = 16384
ATTN_SCALE = HEAD_DIM ** -0.5

LANES = 128
SUBLANES = 8
V7X_VMEM_LIMIT_BYTES = 56 * 1024 * 1024

ROW_TILE = 1024
COL_TILE = 1024
POOL_COL_TILE = 512
OUT_ROW_TILE = 512
ATTN_BLOCK = 128
COMBINE_ROW_TILE = 256
SAMPLE_CHUNK_ROWS = 256
HALO = POOL_PREFIX + 1
PAD = SUBLANES


def _params(n_axes):
    return pltpu.CompilerParams(
        dimension_semantics=("arbitrary",) * n_axes,
        vmem_limit_bytes=V7X_VMEM_LIMIT_BYTES,
    )


def _col_tile(ncols, col0, cap):
    tn = min(ncols, cap) // LANES * LANES
    while ncols % tn or col0 % tn:
        tn -= LANES
    return tn


def _silu(x):
    return x * (1.0 / (1.0 + jnp.exp(-x)))


def _rms_scale(x):
    return lax.rsqrt(jnp.mean(x * x, axis=-1, keepdims=True) + RMS_EPS)


def _rms_kernel(x_ref, g_ref, h_ref):
    x = x_ref[...]
    h_ref[...] = (x * _rms_scale(x) * g_ref[...]).astype(h_ref.dtype)


def _rms_pre(x, g):
    m, d = x.shape
    tm = min(m, OUT_ROW_TILE)
    return pl.pallas_call(
        _rms_kernel,
        out_shape=jax.ShapeDtypeStruct((m, d), BF16),
        grid=(m // tm,),
        in_specs=[pl.BlockSpec((tm, d), lambda i: (i, 0)), pl.BlockSpec((1, d), lambda i: (0, 0))],
        out_specs=pl.BlockSpec((tm, d), lambda i: (i, 0)),
        compiler_params=_params(1),
        name="rms_pre",
    )(x, g.reshape(1, d))


def _proj_kernel(h_ref, w_ref, *refs, act, dil, tail_rows, tail_first_tile, tiles_per_batch):
    refs = list(refs)
    out_ref = refs.pop(0)
    tail_ref = refs.pop(0) if tail_rows else None
    wbf_ref = refs.pop(0)
    acc_ref = refs.pop(0) if dil > 1 else None
    i = pl.program_id(1)

    @pl.when(i == 0)
    def _():
        wbf_ref[...] = w_ref[...].astype(BF16)

    acc = jnp.dot(h_ref[...], wbf_ref[...], preferred_element_type=F32)
    tm, tn = acc.shape
    if act == "silu":
        acc = _silu(acc)

    if dil == 1 and len(out_ref.shape) == 3:
        out_ref[0] = acc.astype(out_ref.dtype)
    elif dil == 1:
        out_ref[...] = acc.astype(out_ref.dtype)
    else:
        for c in range(tn // LANES):
            acc_ref[c] = acc[:, c * LANES:(c + 1) * LANES]
        for r in range(dil):
            for c in range(tn // LANES):
                out_ref[r, :, c * LANES:(c + 1) * LANES] = (
                    acc_ref[c, pl.ds(r, tm // dil, stride=dil), :].astype(out_ref.dtype))

    if tail_rows:
        heads = tn // HEAD_DIM

        @pl.when(i % tiles_per_batch >= tail_first_tile)
        def _():
            for hl in range(heads):
                tail_ref[:, hl, :] = acc[tm - tail_rows:, hl * HEAD_DIM:(hl + 1) * HEAD_DIM]


def _proj(h, w, layer, col0, ncols, *, act=None, out_dtype=BF16, dil=1, rows_per_batch=None,
          tail_window=0, n_heads=None, name="proj"):
    m, k = h.shape
    tm = min(m, ROW_TILE)
    cap = min(COL_TILE, n_heads * HEAD_DIM) if tail_window else COL_TILE
    tn = _col_tile(ncols, col0, cap)
    assert m % tm == 0
    ni, nj = m // tm, ncols // tn
    jb = col0 // tn
    classes = rows_per_batch is not None
    tiles_per_batch = 1
    if classes:
        assert rows_per_batch % tm == 0 and tm % dil == 0
        tiles_per_batch = rows_per_batch // tm
        nb = m // rows_per_batch
        out_shape = [jax.ShapeDtypeStruct((nb, dil, rows_per_batch // dil, ncols), out_dtype)]
        out_specs = [pl.BlockSpec((None, dil, tm // dil, tn),
                                  lambda j, i: (i // tiles_per_batch, 0, i % tiles_per_batch, j))]
    else:
        assert dil == 1
        out_shape = [jax.ShapeDtypeStruct((m, ncols), out_dtype)]
        out_specs = [pl.BlockSpec((tm, tn), lambda j, i: (i, j))]
    scratch = [pltpu.VMEM((k, tn), BF16)]
    if dil > 1:
        scratch.append(pltpu.VMEM((tn // LANES, tm, LANES), F32))
    tail_rows = tail_first_tile = 0
    if tail_window:
        assert classes and tail_window <= rows_per_batch
        heads_per_block = tn // HEAD_DIM
        width = n_heads * HEAD_DIM
        assert width % tn == 0 and (heads_per_block == n_heads or heads_per_block % SUBLANES == 0)
        blocks_per_part = width // tn
        tail_rows = min(tail_window, tm)
        assert tail_window % tail_rows == 0
        tail_first_tile = tiles_per_batch - tail_window // tail_rows
        out_shape.append(jax.ShapeDtypeStruct(
            (nb, tail_window, ncols // width, n_heads, HEAD_DIM), F32))
        out_specs.append(pl.BlockSpec(
            (None, tail_rows, None, heads_per_block, HEAD_DIM),
            lambda j, i: (i // tiles_per_batch,
                          jnp.maximum(i % tiles_per_batch - tail_first_tile, 0),
                          j // blocks_per_part, j % blocks_per_part, 0)))
    kern = functools.partial(_proj_kernel, act=act, dil=dil, tail_rows=tail_rows,
                             tail_first_tile=tail_first_tile, tiles_per_batch=tiles_per_batch)
    outs = pl.pallas_call(
        kern,
        out_shape=out_shape,
        grid=(nj, ni),
        in_specs=[pl.BlockSpec((tm, k), lambda j, i: (i, 0)),
                  pl.BlockSpec((None, k, tn), lambda j, i: (layer, 0, jb + j))],
        out_specs=out_specs,
        scratch_shapes=scratch,
        compiler_params=_params(2),
        name=name,
    )(h, w)
    return outs if tail_window else outs[0]


def _pool_kernel(h_ref, w_ref, prefix_ref, mixed_ref, state_ref, wbf_ref, ext_ref, lvl_a, lvl_b, *,
                 seg, nseg, tiles_per_batch, blocks_per_group, pos0):
    j = pl.program_id(0)
    i = pl.program_id(1)
    tile_in_batch = i % tiles_per_batch

    @pl.when(i == 0)
    def _():
        wbf_ref[...] = w_ref[...].astype(BF16)

    u = jnp.dot(h_ref[...], wbf_ref[...], preferred_element_type=F32)
    tn = u.shape[1]
    end = PAD + HALO + seg
    zeros = jnp.zeros((PAD, tn), F32)
    for s in range(nseg):
        if nseg == 1:
            @pl.when(tile_in_batch == 0)
            def _():
                ext_ref[0, PAD:PAD + HALO, :] = prefix_ref[...]

            @pl.when(tile_in_batch != 0)
            def _():
                ext_ref[0, PAD:PAD + HALO, :] = ext_ref[0, end - HALO:end, :]
        else:
            ext_ref[s, PAD:PAD + HALO, :] = prefix_ref[s]
        ext_ref[s, 0:PAD, :] = zeros
        lvl_a[s, 0:PAD, :] = zeros
        lvl_b[s, 0:PAD, :] = zeros
        ext_ref[s, PAD + HALO:end, :] = u[s * seg:(s + 1) * seg]
        if nseg == 1:
            state_ref[...] = ext_ref[0, end - HALO:end, :]
        else:
            state_ref[s] = ext_ref[s, end - HALO:end, :]

    group = j // blocks_per_group
    first = PAD + HALO
    for gi, w in enumerate(POOL_WINDOWS):
        @pl.when(group == gi)
        def _(w=w):
            for s in range(nseg):
                src, bufs, k = ext_ref, (lvl_a, lvl_b), 1
                while 2 * k < w:
                    dst = bufs[0]
                    dst[s, PAD:end, :] = src[s, PAD:end, :] + src[s, PAD - k:end - k, :]
                    src, bufs, k = dst, (bufs[1], bufs[0]), 2 * k
                win_sum = src[s, first:end, :] + src[s, first - k:end - k, :]
                if pos0 + 1 >= w:
                    inv_cnt = 1.0 / w
                else:
                    row = lax.broadcasted_iota(jnp.int32, (seg, 1), 0)
                    pos = pos0 + tile_in_batch * seg + row
                    inv_cnt = 1.0 / jnp.minimum(pos + 1, w).astype(F32)
                mixed = win_sum * inv_cnt - ext_ref[s, first:end, :]
                mixed_ref[s * seg:(s + 1) * seg, :] = mixed.astype(mixed_ref.dtype)


def _pool_u(h, w_in, layer, prefix16, rows_per_batch, pos0):
    m, k = h.shape
    pw = w_in.shape[2] // 2
    nb = m // rows_per_batch
    group_cols = pw // len(POOL_WINDOWS)
    tm = min(m, ROW_TILE)
    tn = min(group_cols, POOL_COL_TILE)
    assert m % tm == 0 and group_cols % tn == 0
    if rows_per_batch >= tm:
        assert rows_per_batch % tm == 0
        seg, nseg, tiles_per_batch = tm, 1, rows_per_batch // tm
        prefix_spec = pl.BlockSpec((None, HALO, tn), lambda j, i: (i // tiles_per_batch, 0, j))
    else:
        assert tm % rows_per_batch == 0
        seg, nseg, tiles_per_batch = rows_per_batch, tm // rows_per_batch, 1
        prefix_spec = pl.BlockSpec((nseg, HALO, tn), lambda j, i: (i, 0, j))
    kern = functools.partial(_pool_kernel, seg=seg, nseg=nseg, tiles_per_batch=tiles_per_batch,
                             blocks_per_group=group_cols // tn, pos0=pos0)
    rows = PAD + HALO + seg
    return pl.pallas_call(
        kern,
        out_shape=[jax.ShapeDtypeStruct((m, pw), BF16), jax.ShapeDtypeStruct((nb, HALO, pw), F32)],
        grid=(pw // tn, m // tm),
        in_specs=[pl.BlockSpec((tm, k), lambda j, i: (i, 0)),
                  pl.BlockSpec((None, k, tn), lambda j, i: (layer, 0, j)),
                  prefix_spec],
        out_specs=[pl.BlockSpec((tm, tn), lambda j, i: (i, j)), prefix_spec],
        scratch_shapes=[pltpu.VMEM((k, tn), BF16)] + [pltpu.VMEM((nseg, rows, tn), F32)] * 3,
        compiler_params=_params(2),
        name="pool_u",
    )(h, w_in, prefix16)


def _grouped_kernel(x_ref, w_ref, scale_ref, sz_ref, out_ref, wbf_ref):
    @pl.when(pl.program_id(1) == 0)
    def _():
        wbf_ref[...] = w_ref[...].astype(BF16)

    y = jnp.dot(x_ref[...], wbf_ref[...], preferred_element_type=F32)
    out_ref[...] = (y * scale_ref[...] * sz_ref[...]).astype(out_ref.dtype)


def _grouped(mixed, w_grp, layer, scale, sz):
    m, pw = mixed.shape
    gc = w_grp.shape[-1]
    tm = min(m, ROW_TILE)
    tn = min(gc, COL_TILE)
    per_group = gc // tn
    return pl.pallas_call(
        _grouped_kernel,
        out_shape=jax.ShapeDtypeStruct((m, pw), BF16),
        grid=(pw // tn, m // tm),
        in_specs=[pl.BlockSpec((tm, gc), lambda j, i: (i, j // per_group)),
                  pl.BlockSpec((None, None, gc, tn), lambda j, i: (layer, j // per_group, 0, j % per_group)),
                  pl.BlockSpec((1, tn), lambda j, i: (0, j)),
                  pl.BlockSpec((tm, tn), lambda j, i: (i, j))],
        out_specs=pl.BlockSpec((tm, tn), lambda j, i: (i, j)),
        scratch_shapes=[pltpu.VMEM((gc, tn), BF16)],
        compiler_params=_params(2),
        name="pool_grouped",
    )(mixed, w_grp, scale.reshape(1, pw), sz)


def _out_kernel(t_ref, w_ref, x_ref, gpost_ref, *refs, has_next):
    if has_next:
        gnext_ref, xo_ref, ho_ref = refs
    else:
        (xo_ref,) = refs
    o = jnp.dot(t_ref[...], w_ref[...], preferred_element_type=F32)
    xn = x_ref[...] + o * _rms_scale(o) * gpost_ref[...]
    xo_ref[...] = xn
    if has_next:
        ho_ref[...] = (xn * _rms_scale(xn) * gnext_ref[...]).astype(ho_ref.dtype)


def _out_proj(t, w_bf, layer, x, g_post, g_next):
    m, kt = t.shape
    d = x.shape[1]
    tm = min(m, OUT_ROW_TILE)
    has_next = g_next is not None
    row = lambda i: (i, 0)
    vec = pl.BlockSpec((1, d), lambda i: (0, 0))
    in_specs = [pl.BlockSpec((tm, kt), row),
                pl.BlockSpec((None, kt, d), lambda i: (layer, 0, 0), pipeline_mode=pl.Buffered(1)),
                pl.BlockSpec((tm, d), row), vec]
    args = [t, w_bf, x, g_post.reshape(1, d)]
    out_shape = [jax.ShapeDtypeStruct((m, d), F32)]
    out_specs = [pl.BlockSpec((tm, d), row)]
    if has_next:
        in_specs.append(vec)
        args.append(g_next.reshape(1, d))
        out_shape.append(jax.ShapeDtypeStruct((m, d), BF16))
        out_specs.append(pl.BlockSpec((tm, d), row))
    outs = pl.pallas_call(
        functools.partial(_out_kernel, has_next=has_next),
        out_shape=out_shape,
        grid=(m // tm,),
        in_specs=in_specs,
        out_specs=out_specs,
        compiler_params=_params(1),
        name="out_proj",
    )(*args)
    return (outs[0], outs[1]) if has_next else (outs[0], None)


def _qkt(q, k):
    return lax.dot_general(q, k, (((1,), (1,)), ((), ())), preferred_element_type=F32)


def _attn_prompt_kernel(q_ref, kc_ref, vc_ref, kp_ref, vp_ref, o_ref, lse_ref, s_ref, p_ref, *, n_heads):
    b = pl.program_id(2)
    nq = ATTN_BLOCK
    row = lax.broadcasted_iota(jnp.int32, (nq, nq), 0)
    col = lax.broadcasted_iota(jnp.int32, (nq, nq), 1)
    mask_cur = col <= row
    mask_prev = (col >= row) & (b > 0)
    for h in range(n_heads):
        sl = slice(h * HEAD_DIM, (h + 1) * HEAD_DIM)
        q = q_ref[:, sl]
        s_ref[h, :, 0:nq] = jnp.where(mask_prev, _qkt(q, kp_ref[:, sl]) * ATTN_SCALE, -jnp.inf)
        s_ref[h, :, nq:2 * nq] = jnp.where(mask_cur, _qkt(q, kc_ref[:, sl]) * ATTN_SCALE, -jnp.inf)
    lane = lax.broadcasted_iota(jnp.int32, (nq, LANES), 1)
    lse_tile = jnp.zeros((nq, LANES), F32)
    dens = []
    for h in range(n_heads):
        s = s_ref[h]
        mx = jnp.max(s, axis=-1, keepdims=True)
        p = jnp.exp(s - mx)
        den = jnp.sum(p, axis=-1, keepdims=True)
        p_ref[h] = p.astype(BF16)
        dens.append(den)
        lse_tile = jnp.where(lane == h, mx + jnp.log(den), lse_tile)
    lse_ref[...] = lse_tile
    for h in range(n_heads):
        sl = slice(h * HEAD_DIM, (h + 1) * HEAD_DIM)
        o = (jnp.dot(p_ref[h, :, 0:nq], vp_ref[:, sl], preferred_element_type=F32)
             + jnp.dot(p_ref[h, :, nq:2 * nq], vc_ref[:, sl], preferred_element_type=F32))
        o_ref[:, sl] = o / dens[h]


def _attn_prompt(q, kv, n_heads):
    nb, dil, lc, d = q.shape
    assert lc % ATTN_BLOCK == 0
    blk = (None, None, ATTN_BLOCK, d)
    prev = lambda b: jnp.maximum(b - 1, 0)
    return pl.pallas_call(
        functools.partial(_attn_prompt_kernel, n_heads=n_heads),
        out_shape=[jax.ShapeDtypeStruct((nb, dil, lc, d), F32),
                   jax.ShapeDtypeStruct((nb, dil, lc, LANES), F32)],
        grid=(nb, dil, lc // ATTN_BLOCK),
        in_specs=[pl.BlockSpec(blk, lambda n, r, b: (n, r, b, 0)),
                  pl.BlockSpec(blk, lambda n, r, b: (n, r, b, 0)),
                  pl.BlockSpec(blk, lambda n, r, b: (n, r, b, 1)),
                  pl.BlockSpec(blk, lambda n, r, b: (n, r, prev(b), 0)),
                  pl.BlockSpec(blk, lambda n, r, b: (n, r, prev(b), 1))],
        out_specs=[pl.BlockSpec(blk, lambda n, r, b: (n, r, b, 0)),
                   pl.BlockSpec((None, None, ATTN_BLOCK, LANES), lambda n, r, b: (n, r, b, 0))],
        scratch_shapes=[pltpu.VMEM((n_heads, ATTN_BLOCK, 2 * ATTN_BLOCK), F32),
                        pltpu.VMEM((n_heads, ATTN_BLOCK, 2 * ATTN_BLOCK), BF16)],
        compiler_params=_params(3),
        name="attn_prompt",
    )(q, kv, kv, kv, kv)


def _attn_sample_kernel(q_ref, kn_ref, vn_ref, new_ref, buf_ref, next_ref, o_ref, lse_ref, out_ref,
                        bias_ref, s_ref, m_ref, l_ref, acc_ref, *, dil, n_heads, n_chunks):
    c = pl.program_id(1)
    rows, t_len = buf_ref.shape[0], new_ref.shape[0]
    th = t_len * n_heads
    log_h = n_heads.bit_length() - 1
    last = n_chunks - 1

    def pair_mask(n_rows, edge=None):
        rho = lax.broadcasted_iota(jnp.int32, (n_rows * n_heads, th), 0)
        lam = lax.broadcasted_iota(jnp.int32, (n_rows * n_heads, th), 1)
        back = (lam >> log_h) - (rho >> log_h)
        ok = (((rho ^ lam) & (n_heads - 1)) == 0) & ((back & (dil - 1)) == 0)
        if edge == "far":
            ok = ok & (back <= 0)
        if edge == "causal":
            ok = ok & (back >= 0)
        return ok

    @pl.when((pl.program_id(0) == 0) & (c == 0))
    def _():
        bias_ref[...] = jnp.where(pair_mask(rows), 0.0, -jnp.inf)

    @pl.when(c == 0)
    def _():
        m_ref[...] = jnp.full(m_ref.shape, -jnp.inf, F32)
        l_ref[...] = jnp.zeros(l_ref.shape, F32)
        acc_ref[...] = jnp.zeros(acc_ref.shape, F32)

    q = q_ref[...].astype(BF16)

    def update(s, v):
        m_prev = m_ref[0:1, :]
        m_new = jnp.maximum(m_prev, jnp.max(s, axis=0, keepdims=True))
        p = jnp.exp(s - m_new)
        alpha = jnp.exp(m_prev - m_new)
        l_ref[0:1, :] = alpha * l_ref[0:1, :] + jnp.sum(p, axis=0, keepdims=True)
        pv = lax.dot_general(v, p.astype(BF16), (((0,), (0,)), ((), ())), preferred_element_type=F32)
        acc_ref[...] = alpha * acc_ref[...] + pv
        m_ref[0:1, :] = m_new

    k2d = buf_ref[:, 0].reshape(rows * n_heads, HEAD_DIM).astype(BF16)
    v2d = buf_ref[:, 1].reshape(rows * n_heads, HEAD_DIM).astype(BF16)
    s_ref[...] = _qkt(k2d, q) * ATTN_SCALE + bias_ref[...]

    @pl.when(c == 0)
    def _():
        top = t_len * n_heads
        s_ref[0:top, :] = jnp.where(pair_mask(t_len, "far"), s_ref[0:top, :], -jnp.inf)

    update(s_ref[...], v2d)

    out_ref[0:rows - t_len] = buf_ref[t_len:rows]

    @pl.when(c != last)
    def _():
        out_ref[rows - t_len:rows] = next_ref[...]

    @pl.when(c == last)
    def _():
        out_ref[rows - t_len:rows] = new_ref[...]
        s_new = _qkt(kn_ref[...].astype(BF16), q) * ATTN_SCALE
        update(jnp.where(pair_mask(t_len, "causal"), s_new, -jnp.inf), vn_ref[...].astype(BF16))
        l_fin = l_ref[0:1, :]
        o_ref[...] = jnp.transpose(acc_ref[...] / l_fin)
        lse_ref[...] = jnp.broadcast_to(m_ref[0:1, :] + jnp.log(l_fin), lse_ref.shape)


def _attn_sample(q, kv_new, cache, win, dil):
    nb, t_len, n_heads, _ = q.shape
    n_buf = cache.shape[1]
    th = t_len * n_heads
    rows = min(n_buf, SAMPLE_CHUNK_ROWS)
    assert n_buf == win and n_buf % rows == 0 and rows % dil == 0 and rows % t_len == 0
    assert dil & (dil - 1) == 0 and n_heads & (n_heads - 1) == 0 and rows > t_len
    n_chunks = n_buf // rows
    flat = lambda a: a.reshape(nb, th, HEAD_DIM)
    small = pl.BlockSpec((None, th, HEAD_DIM), lambda n, c: (n, 0, 0))
    per_row = rows // t_len
    next_blk = lambda n, c: (n, jnp.minimum((c + 1) * per_row, n_buf // t_len - 1), 0, 0, 0)
    o, lse, new_buf = pl.pallas_call(
        functools.partial(_attn_sample_kernel, dil=dil, n_heads=n_heads, n_chunks=n_chunks),
        out_shape=[jax.ShapeDtypeStruct((nb, th, HEAD_DIM), F32),
                   jax.ShapeDtypeStruct((nb, SUBLANES, th), F32),
                   jax.ShapeDtypeStruct(cache.shape, cache.dtype)],
        grid=(nb, n_chunks),
        in_specs=[small, small, small,
                  pl.BlockSpec((None, t_len, 2, n_heads, HEAD_DIM), lambda n, c: (n, 0, 0, 0, 0)),
                  pl.BlockSpec((None, rows, 2, n_heads, HEAD_DIM), lambda n, c: (n, c, 0, 0, 0)),
                  pl.BlockSpec((None, t_len, 2, n_heads, HEAD_DIM), next_blk)],
        out_specs=[small,
                   pl.BlockSpec((None, SUBLANES, th), lambda n, c: (n, 0, 0)),
                   pl.BlockSpec((None, rows, 2, n_heads, HEAD_DIM), lambda n, c: (n, c, 0, 0, 0))],
        scratch_shapes=[pltpu.VMEM((rows * n_heads, th), F32), pltpu.VMEM((rows * n_heads, th), F32),
                        pltpu.VMEM((SUBLANES, th), F32), pltpu.VMEM((SUBLANES, th), F32),
                        pltpu.VMEM((HEAD_DIM, th), F32)],
        compiler_params=_params(2),
        name="attn_sample",
    )(flat(q), flat(kv_new[:, :, 0]), flat(kv_new[:, :, 1]), kv_new, cache, cache)
    lse = lse[:, 0, :].reshape(nb, t_len, n_heads)
    lse = jnp.pad(lse, ((0, 0), (0, 0), (0, LANES - n_heads)))
    return o.reshape(nb, t_len, n_heads, HEAD_DIM), lse, new_buf


def _combine_kernel(*refs, dils, n_heads):
    ng = len(dils)
    o_refs, lse_refs = refs[:ng], refs[ng:2 * ng]
    gate_ref, t_ref, o_pos, lse_pos = refs[2 * ng:]
    tm = t_ref.shape[0]
    for g, dil in enumerate(dils):
        for r in range(dil):
            rows = pl.ds(r, tm // dil, stride=dil) if dil > 1 else slice(None)
            lse_pos[g, rows, :] = lse_refs[g][r]
            for h in range(n_heads):
                o_pos[g, h, rows, :] = o_refs[g][r, :, h * HEAD_DIM:(h + 1) * HEAD_DIM]
    lses = [lse_pos[g] for g in range(ng)]
    mx = functools.reduce(jnp.maximum, lses)
    es = [jnp.exp(l - mx) for l in lses]
    den = functools.reduce(jnp.add, es)
    ws = [e / den for e in es]
    for h in range(n_heads):
        sl = slice(h * HEAD_DIM, (h + 1) * HEAD_DIM)
        mix = functools.reduce(jnp.add, [ws[g][:, h:h + 1] * o_pos[g, h] for g in range(ng)])
        t_ref[:, sl] = (mix * gate_ref[:, sl]).astype(t_ref.dtype)


def _combine(os_, lses, gate, n_heads):
    nb, seq, d = gate.shape
    dils = tuple(o.shape[1] for o in os_)
    tm = min(seq, COMBINE_ROW_TILE)
    assert seq % tm == 0 and all(tm % dl == 0 for dl in dils)
    in_specs = [pl.BlockSpec((None, dl, tm // dl, d), lambda n, i: (n, 0, i, 0)) for dl in dils]
    in_specs += [pl.BlockSpec((None, dl, tm // dl, LANES), lambda n, i: (n, 0, i, 0)) for dl in dils]
    in_specs.append(pl.BlockSpec((None, tm, d), lambda n, i: (n, i, 0)))
    return pl.pallas_call(
        functools.partial(_combine_kernel, dils=dils, n_heads=n_heads),
        out_shape=jax.ShapeDtypeStruct((nb, seq, d), BF16),
        grid=(nb, seq // tm),
        in_specs=in_specs,
        out_specs=pl.BlockSpec((None, tm, d), lambda n, i: (n, i, 0)),
        scratch_shapes=[pltpu.VMEM((len(dils), n_heads, tm, HEAD_DIM), F32),
                        pltpu.VMEM((len(dils), tm, LANES), F32)],
        compiler_params=_params(2),
        name="attn_combine",
    )(*os_, *lses, gate)


def _pool_layer(h, x, rows_per_batch, prefix16, pos0, layer, w_in, w_grp, scale, w_out_bf, g_post, g_next):
    pw = w_in.shape[2] // 2
    mixed, state16 = _pool_u(h, w_in, layer, prefix16, rows_per_batch, pos0)
    sz = _proj(h, w_in, layer, pw, pw, act="silu", name="pool_z")
    t = _grouped(mixed, w_grp, layer, scale[layer], sz)
    x, h = _out_proj(t, w_out_bf, layer, x, g_post, g_next)
    return x, h, state16[:, 1:]


def _attn_layer_prompt(h, x, nb, seq, n_heads, layer, w_in, w_out_bf, g_post, g_next):
    d = x.shape[1]
    os_, lses, bufs = [], [], []
    for gi, (win, dil) in enumerate(DILATED_GROUPS):
        assert win // dil == ATTN_BLOCK
        base = gi * 3 * d
        q = _proj(h, w_in, layer, base, d, dil=dil, rows_per_batch=seq, name="attn_q")
        kv, tail = _proj(h, w_in, layer, base + d, 2 * d, dil=dil, rows_per_batch=seq,
                         tail_window=min(win, seq), n_heads=n_heads, name="attn_kv")
        o, lse = _attn_prompt(q, kv, n_heads)
        os_.append(o)
        lses.append(lse)
        bufs.append(tail)
    gate = _proj(h, w_in, layer, len(DILATED_GROUPS) * 3 * d, d, act="silu", name="attn_gate")
    t = _combine(os_, lses, gate.reshape(nb, seq, d), n_heads)
    x, h = _out_proj(t.reshape(nb * seq, d), w_out_bf, layer, x, g_post, g_next)
    return x, h, bufs


def _attn_layer_sample(h, x, nb, t_len, n_heads, caches, layer, w_in, w_out_bf, g_post, g_next):
    d = x.shape[1]
    ng = len(DILATED_GROUPS)
    proj = _proj(h, w_in, layer, 0, ng * 3 * d, out_dtype=F32, name="attn_qkv_sample")
    gate = _proj(h, w_in, layer, ng * 3 * d, d, act="silu", name="attn_gate")
    os_, lses, bufs = [], [], []
    for gi, (win, dil) in enumerate(DILATED_GROUPS):
        base = gi * 3 * d
        q = proj[:, base:base + d].reshape(nb, t_len, n_heads, HEAD_DIM)
        kv_new = proj[:, base + d:base + 3 * d].reshape(nb, t_len, 2, n_heads, HEAD_DIM)
        o, lse, buf = _attn_sample(q, kv_new, caches[gi], win, dil)
        os_.append(o.reshape(nb, 1, t_len, d))
        lses.append(lse.reshape(nb, 1, t_len, LANES))
        bufs.append(buf)
    t = _combine(os_, lses, gate.reshape(nb, t_len, d), n_heads)
    x, h = _out_proj(t.reshape(nb * t_len, d), w_out_bf, layer, x, g_post, g_next)
    return x, h, bufs


def kernel(x_prompt, x_sample, state_pool_l0, cache_kv_l1_w128, cache_kv_l1_w512, cache_kv_l1_w2048,
           state_pool_l2, cache_kv_l3_w128, cache_kv_l3_w512, cache_kv_l3_w2048,
           norm_pre, norm_post, w_in_pool, w_grp_pool, scale_pool, w_out_pool, w_in_attn, w_out_attn):
    nbp, seq, d = x_prompt.shape
    nbs, t_len, _ = x_sample.shape
    depth = norm_pre.shape[0]
    n_heads = d // HEAD_DIM
    pw = w_in_pool.shape[2] // 2
    pool_states = {0: state_pool_l0, 2: state_pool_l2}
    attn_caches = {1: (cache_kv_l1_w128, cache_kv_l1_w512, cache_kv_l1_w2048),
                   3: (cache_kv_l3_w128, cache_kv_l3_w512, cache_kv_l3_w2048)}
    w_out_pool_bf = w_out_pool.astype(BF16)
    w_out_attn_bf = w_out_attn.astype(BF16)

    xp = x_prompt.reshape(nbp * seq, d)
    xs = x_sample.reshape(nbs * t_len, d)
    hp = _rms_pre(xp, norm_pre[0])
    hs = _rms_pre(xs, norm_pre[0])
    new = {}
    for i in range(depth):
        j = i // 2
        g_post = norm_post[i]
        g_next = norm_pre[i + 1] if i + 1 < depth else None
        if i % 2 == 0:
            zero_prefix = jnp.zeros((nbp, HALO, pw), F32)
            xp, hp, st_p = _pool_layer(hp, xp, seq, zero_prefix, 0, j, w_in_pool, w_grp_pool,
                                       scale_pool, w_out_pool_bf, g_post, g_next)
            prefix = jnp.pad(pool_states[i], ((0, 0), (HALO - POOL_PREFIX, 0), (0, 0)))
            xs, hs, st_s = _pool_layer(hs, xs, t_len, prefix, PAST_LEN, j, w_in_pool, w_grp_pool,
                                       scale_pool, w_out_pool_bf, g_post, g_next)
            new["pool_l%d_prompt" % i], new["pool_l%d_sample" % i] = st_p, st_s
        else:
            xp, hp, bufs_p = _attn_layer_prompt(hp, xp, nbp, seq, n_heads, j, w_in_attn,
                                                w_out_attn_bf, g_post, g_next)
            xs, hs, bufs_s = _attn_layer_sample(hs, xs, nbs, t_len, n_heads, attn_caches[i], j,
                                                w_in_attn, w_out_attn_bf, g_post, g_next)
            for (win, _), bp, bs in zip(DILATED_GROUPS, bufs_p, bufs_s):
                new["kv_l%d_w%d_prompt" % (i, win)] = bp
                new["kv_l%d_w%d_sample" % (i, win)] = bs
    outs = [xp.reshape(nbp, seq, d), xs.reshape(nbs, t_len, d)]
    for i in range(depth):
        if i % 2 == 0:
            outs += [new["pool_l%d_prompt" % i], new["pool_l%d_sample" % i]]
        else:
            for win, _ in DILATED_GROUPS:
                outs += [new["kv_l%d_w%d_prompt" % (i, win)], new["kv_l%d_w%d_sample" % (i, win)]]
    return tuple(outs)
```
